```python
import jax, jax.numpy as jnp
from jax import lax
import numpy as np

D_MODEL = 2048
BATCH = 32
SEQ = 256
DEPTH = 2
DEC_BATCH = 4
DEC_SEQ = 1024
PAST_LEN = 256

GRID_W = 64
N_BRANCH = 4
BRANCH_W = D_MODEL // 2
QK_NOPE = 128
QK_ROPE = 64
V_HEAD = 128
N_HEADS = BRANCH_W // V_HEAD
Q_LORA = 512
KV_LORA = 256
ROPE_BASE = 10000.0
Q_BLOCK = 128
SHORT_K = 3
POOL_WINDOWS = (2, 4, 8, 16)
POOL_GROUPS = 4
POOL_GW = BRANCH_W // POOL_GROUPS
CONF_K = 31
EPS = 1e-6
SEG_SIZES = (Q_LORA, KV_LORA + QK_ROPE, BRANCH_W,
             BRANCH_W, BRANCH_W, BRANCH_W, BRANCH_W,
             BRANCH_W, BRANCH_W,
             2 * BRANCH_W, BRANCH_W,
             N_BRANCH * D_MODEL)
IN_COLS = Q_LORA + KV_LORA + QK_ROPE + 10 * BRANCH_W + N_BRANCH * D_MODEL

kernel_name = "hybrid_mla_conv_pool_conformer_diffusion_step"


def rms_norm(x, w):
    xf = x.astype(jnp.float32)
    y = xf * lax.rsqrt(jnp.mean(xf * xf, axis=-1, keepdims=True) + EPS)
    return (y * w.astype(jnp.float32)).astype(x.dtype)


def layer_norm(x, w, b):
    xf = x.astype(jnp.float32)
    mu = jnp.mean(xf, axis=-1, keepdims=True)
    var = jnp.mean(jnp.square(xf - mu), axis=-1, keepdims=True)
    y = (xf - mu) * lax.rsqrt(var + EPS)
    return (y * w.astype(jnp.float32) + b.astype(jnp.float32)).astype(x.dtype)


def depthwise_conv(x, w, b):
    y = lax.conv_general_dilated(x, w[:, None, :].astype(x.dtype), window_strides=(1,), padding='SAME',
                                 dimension_numbers=('NWC', 'WIO', 'NWC'), feature_group_count=x.shape[-1])
    return y + b.astype(x.dtype)


def grid_rope_tables(n_tokens):
    rows = n_tokens // GRID_W
    r, col = jnp.meshgrid(jnp.arange(rows, dtype=jnp.float32), jnp.arange(GRID_W, dtype=jnp.float32), indexing='ij')
    half = QK_ROPE // 2
    inv = ROPE_BASE ** (-jnp.arange(0, half, 2, dtype=jnp.float32) / half)
    ang = jnp.stack([r.reshape(-1)[:, None] * inv, col.reshape(-1)[:, None] * inv], axis=1)
    return jnp.cos(ang), jnp.sin(ang)


def apply_rope(x, cos, sin):
    xr = x.reshape(x.shape[:-1] + (2, 2, QK_ROPE // 4))
    x1, x2 = xr[..., 0, :], xr[..., 1, :]
    cos = cos.astype(x.dtype)
    sin = sin.astype(x.dtype)
    out = jnp.stack([x1 * cos - x2 * sin, x2 * cos + x1 * sin], axis=-2)
    return out.reshape(x.shape)


def block_attention(q, k, v):
    B, Lq, H, Dk = q.shape
    nb = Lq // Q_BLOCK
    scale = Dk ** -0.5
    qb = q.reshape(B, nb, Q_BLOCK, H, Dk).swapaxes(0, 1)

    def one_block(qblk):
        s = jnp.einsum('bqhd,bkhd->bhqk', qblk, k, preferred_element_type=jnp.float32) * scale
        p = jax.nn.softmax(s, axis=-1).astype(v.dtype)
        return jnp.einsum('bhqk,bkhd->bqhd', p, v)

    o = lax.map(one_block, qb)
    return o.swapaxes(0, 1).reshape(B, Lq, H, v.shape[-1])


def multiscale_pool(xp, pool_w, pool_scale):
    B, L, _ = xp.shape
    xf = xp.astype(jnp.float32)
    cs = jnp.concatenate([jnp.zeros((B, 1, BRANCH_W), jnp.float32), jnp.cumsum(xf, axis=1)], axis=1)
    t = jnp.arange(L)
    groups = []
    for g, w in enumerate(POOL_WINDOWS):
        lo = jnp.clip(t - w // 2, 0, L)
        hi = jnp.clip(t - w // 2 + w, 0, L)
        sl = slice(g * POOL_GW, (g + 1) * POOL_GW)
        csg = cs[..., sl]
        mean = (csg[:, hi] - csg[:, lo]) / (hi - lo).astype(jnp.float32)[None, :, None]
        groups.append(mean - xf[..., sl])
    pooled = jnp.stack(groups, axis=2).astype(xp.dtype)
    y = jnp.einsum('blgc,gcd->blgd', pooled, pool_w).reshape(B, L, BRANCH_W)
    return y * pool_scale


def mixer(h, p, ctx_kv, rope):
    B, L, _ = h.shape
    proj = h @ p['w_in']
    split_pts = [int(s) for s in np.cumsum(SEG_SIZES)[:-1]]
    (q_a, kv_a, g_a, b_gate, c_gate, x_conv, g_b, x_pool, g_c, glu_in, g_d, merge_logits) = jnp.split(proj, split_pts, axis=-1)

    q = (rms_norm(q_a, p['q_norm_w']) @ p['w_qb']).reshape(B, L, N_HEADS, QK_NOPE + QK_ROPE)
    q_nope, q_pe = q[..., :QK_NOPE], q[..., QK_NOPE:]
    c_kv = rms_norm(kv_a[..., :KV_LORA], p['kv_norm_w'])
    k_pe = kv_a[..., KV_LORA:]
    if ctx_kv is None:
        new_kv = jnp.concatenate([c_kv, k_pe], axis=-1)
        kv_all = new_kv
    else:
        cos, sin = rope
        q_pe = apply_rope(q_pe, cos[:, None], sin[:, None])
        k_pe = apply_rope(k_pe, cos, sin)
        kv_all = jnp.concatenate([ctx_kv.astype(h.dtype), jnp.concatenate([c_kv, k_pe], axis=-1)], axis=1)
        new_kv = None
    Lk = kv_all.shape[1]
    kv_up = (kv_all[..., :KV_LORA] @ p['w_kvb']).reshape(B, Lk, N_HEADS, QK_NOPE + V_HEAD)
    k_nope, v = kv_up[..., :QK_NOPE], kv_up[..., QK_NOPE:]
    k_rot = jnp.broadcast_to(kv_all[:, :, None, KV_LORA:], (B, Lk, N_HEADS, QK_ROPE))
    k = jnp.concatenate([k_nope, k_rot], axis=-1)
    qf = jnp.concatenate([q_nope, q_pe], axis=-1)
    y_a = block_attention(qf, k, v).reshape(B, L, N_HEADS * V_HEAD) * jax.nn.silu(g_a)

    y_b = b_gate * depthwise_conv(c_gate * x_conv, p['conv3_w'], p['conv3_b']) * jax.nn.silu(g_b)

    y_c = multiscale_pool(x_pool, p['pool_w'], p['pool_scale']) * jax.nn.silu(g_c)

    u = glu_in[..., :BRANCH_W] * jax.nn.sigmoid(glu_in[..., BRANCH_W:])
    u = jax.nn.silu(layer_norm(depthwise_conv(u, p['dw_w'], p['dw_b']), p['cln_w'], p['cln_b']))
    y_d = u * jax.nn.silu(g_d)

    branches = jnp.stack([y_a, y_b, y_c, y_d], axis=2)
    outs = jnp.einsum('blic,icd->blid', branches, p['w_bproj'])
    gates = jax.nn.sigmoid(merge_logits.reshape(B, L, N_BRANCH, D_MODEL))
    merged = jnp.sum(gates * outs, axis=2)
    return merged @ p['w_out'], new_kv


def setup_inputs(seed: int = 0) -> dict:
    key = jax.random.key(seed)
    ks = jax.random.split(key, 26)
    f32 = jnp.float32
    nrm = lambda k, shape, s: jax.random.normal(k, shape, f32) * s
    D, W = D_MODEL, BRANCH_W
    return {
        'x_prompt': nrm(ks[0], (BATCH, SEQ, D), 1.0),
        'x_sample': nrm(ks[1], (DEC_BATCH, DEC_SEQ, D), 1.0),
        'cache_kv': nrm(ks[2], (DEC_BATCH, DEPTH, PAST_LEN, KV_LORA + QK_ROPE), 1.0),
        'c': nrm(ks[3], (DEC_BATCH, D), 1.0),
        'c_ctx': nrm(ks[4], (D,), 1.0),
        'w_ada': nrm(ks[5], (DEPTH, D, 3 * D), 0.5 * D ** -0.5),
        'b_ada': nrm(ks[6], (DEPTH, 3 * D), 0.02),
        'norm_w': 1.0 + nrm(ks[7], (DEPTH, D), 0.05),
        'w_in': nrm(ks[8], (DEPTH, D, IN_COLS), D ** -0.5),
        'q_norm_w': 1.0 + nrm(ks[9], (DEPTH, Q_LORA), 0.05),
        'w_qb': nrm(ks[10], (DEPTH, Q_LORA, N_HEADS * (QK_NOPE + QK_ROPE)), Q_LORA ** -0.5),
        'kv_norm_w': 1.0 + nrm(ks[11], (DEPTH, KV_LORA), 0.05),
        'w_kvb': nrm(ks[12], (DEPTH, KV_LORA, N_HEADS * (QK_NOPE + V_HEAD)), KV_LORA ** -0.5),
        'conv3_w': nrm(ks[13], (DEPTH, SHORT_K, W), SHORT_K ** -0.5),
        'conv3_b': nrm(ks[14], (DEPTH, W), 0.02),
        'pool_w': nrm(ks[15], (DEPTH, POOL_GROUPS, POOL_GW, POOL_GW), POOL_GW ** -0.5),
        'pool_scale': 1.0 + nrm(ks[16], (DEPTH, W), 0.05),
        'dw_w': nrm(ks[17], (DEPTH, CONF_K, W), CONF_K ** -0.5),
        'dw_b': nrm(ks[18], (DEPTH, W), 0.02),
        'cln_w': 1.0 + nrm(ks[19], (DEPTH, W), 0.05),
        'cln_b': nrm(ks[20], (DEPTH, W), 0.02),
        'w_bproj': nrm(ks[21], (DEPTH, N_BRANCH, W, D), W ** -0.5),
        'w_out': nrm(ks[22], (DEPTH, D, D), D ** -0.5),
        'final_norm_w': 1.0 + nrm(ks[23], (D,), 0.05),
    }


def reference(x_prompt, x_sample, cache_kv, c, c_ctx, w_ada, b_ada, norm_w, w_in, q_norm_w, w_qb,
              kv_norm_w, w_kvb, conv3_w, conv3_b, pool_w, pool_scale, dw_w, dw_b, cln_w, cln_b,
              w_bproj, w_out, final_norm_w):
    def layer_params(l):
        return {'w_in': w_in[l], 'q_norm_w': q_norm_w[l], 'w_qb': w_qb[l], 'kv_norm_w': kv_norm_w[l],
                'w_kvb': w_kvb[l], 'conv3_w': conv3_w[l], 'conv3_b': conv3_b[l], 'pool_w': pool_w[l],
                'pool_scale': pool_scale[l], 'dw_w': dw_w[l], 'dw_b': dw_b[l], 'cln_w': cln_w[l],
                'cln_b': cln_b[l], 'w_bproj': w_bproj[l], 'w_out': w_out[l]}

    y_p = x_prompt
    kv_list = []
    for l in range(DEPTH):
        p = layer_params(l)
        shift, scale, gate = jnp.split(jax.nn.silu(c_ctx) @ w_ada[l] + b_ada[l], 3, axis=-1)
        h = rms_norm(y_p, norm_w[l]) * (1 + scale) + shift
        out, kv = mixer(h, p, None, None)
        y_p = y_p + gate * out
        kv_list.append(kv)
    y_prompt = rms_norm(y_p, final_norm_w)
    new_cache_kv = jnp.stack(kv_list, axis=1)

    rope = grid_rope_tables(x_sample.shape[1])
    y_s = x_sample
    for l in range(DEPTH):
        p = layer_params(l)
        mod = (jax.nn.silu(c) @ w_ada[l] + b_ada[l])[:, None, :]
        shift, scale, gate = jnp.split(mod, 3, axis=-1)
        h = rms_norm(y_s, norm_w[l]) * (1 + scale) + shift
        out, _ = mixer(h, p, cache_kv[:, l], rope)
        y_s = y_s + gate * out
    y_sample = rms_norm(y_s, final_norm_w)
    return (y_prompt, y_sample, new_cache_kv)
```

```python
import functools

import numpy as np
import jax
import jax.numpy as jnp
from jax import lax
from jax.experimental import pallas as pl
from jax.experimental.pallas import tpu as pltpu

F32 = jnp.float32
BF16 = jnp.bfloat16

D_MODEL = 2048
DEPTH = 2
GRID_W = 64
N_BRANCH = 4
BRANCH_W = D_MODEL // 2
QK_NOPE = 128
QK_ROPE = 64
V_HEAD = 128
N_HEADS = BRANCH_W // V_HEAD
Q_LORA = 512
KV_LORA = 256
ROPE_BASE = 10000.0
SHORT_K = 3
POOL_WINDOWS = (2, 4, 8, 16)
POOL_GW = BRANCH_W // len(POOL_WINDOWS)
CONF_K = 31
EPS = 1e-6

V7X_LANES = 128
V7X_SUBLANES = 8
V7X_VMEM_BYTES = 64 * 1024 * 1024
VMEM_LIMIT = V7X_VMEM_BYTES - 8 * 1024 * 1024

N_CHUNKS = 11
HEAD_QW = 256
ATT_QB = 256
CH_TILE = 256
ROW_CHUNK = 64
PAD3 = V7X_SUBLANES
PAD31 = 2 * V7X_SUBLANES
PADP = V7X_SUBLANES


def _cparams(sem):
    return pltpu.CompilerParams(dimension_semantics=sem, vmem_limit_bytes=VMEM_LIMIT)


def _silu(x):
    return x * jax.nn.sigmoid(x)


def _dot(a, b):
    return jnp.dot(a, b, preferred_element_type=F32)


def _mod_kernel(c_ref, w_ref, b_ref, o_ref):
    c = c_ref[...]
    s = _silu(c).astype(BF16)
    o_ref[0] = _dot(s, w_ref[0].astype(BF16)) + b_ref[0]


def _modulation(cond, w_ada, b_ada):
    tn = 768
    nt = (3 * D_MODEL) // tn
    return pl.pallas_call(
        _mod_kernel,
        grid=(DEPTH, nt),
        in_specs=[
            pl.BlockSpec((8, D_MODEL), lambda l, n: (0, 0)),
            pl.BlockSpec((1, D_MODEL, tn), lambda l, n: (l, 0, n)),
            pl.BlockSpec((1, 1, tn), lambda l, n: (l, 0, n)),
        ],
        out_specs=pl.BlockSpec((1, 8, tn), lambda l, n: (l, 0, n)),
        out_shape=jax.ShapeDtypeStruct((DEPTH, 8, 3 * D_MODEL), F32),
        compiler_params=_cparams(("arbitrary", "arbitrary")),
        name="adaln_mod",
    )(cond, w_ada, b_ada.reshape(DEPTH, 1, 3 * D_MODEL))


def _inproj_kernel(x_ref, mod_ref, nw_ref, w_ref, h_ref, p_ref, h_scr):
    @pl.when(pl.program_id(1) == 0)
    def _():
        x = x_ref[...]
        ms = jnp.mean(x * x, axis=-1, keepdims=True)
        y = x * lax.rsqrt(ms + EPS) * nw_ref[...]
        shift = mod_ref[0, :, 0:D_MODEL]
        scale = mod_ref[0, :, D_MODEL:2 * D_MODEL]
        hb = (y * (1.0 + scale) + shift).astype(BF16)
        h_scr[...] = hb
        h_ref[...] = hb

    p_ref[...] = _dot(h_scr[...], w_ref[...])


def _inproj(x, mod3, norm_w, w1, rows_per_cond, cond_row0, tm):
    m = x.shape[0]
    tiles_per_cond = rows_per_cond // tm

    def mod_map(i, j):
        return (cond_row0 + i // tiles_per_cond, 0, 0)

    return pl.pallas_call(
        _inproj_kernel,
        grid=(m // tm, N_CHUNKS),
        in_specs=[
            pl.BlockSpec((tm, D_MODEL), lambda i, j: (i, 0)),
            pl.BlockSpec((1, 1, 3 * D_MODEL), mod_map),
            pl.BlockSpec((1, D_MODEL), lambda i, j: (0, 0)),
            pl.BlockSpec((D_MODEL, BRANCH_W), lambda i, j: (0, j)),
        ],
        out_specs=[
            pl.BlockSpec((tm, D_MODEL), lambda i, j: (i, 0)),
            pl.BlockSpec((tm, BRANCH_W), lambda i, j: (i, j)),
        ],
        out_shape=[
            jax.ShapeDtypeStruct((m, D_MODEL), BF16),
            jax.ShapeDtypeStruct((m, N_CHUNKS * BRANCH_W), F32),
        ],
        scratch_shapes=[pltpu.VMEM((tm, D_MODEL), BF16)],
        compiler_params=_cparams(("arbitrary", "arbitrary")),
        name="inproj",
    )(x, mod3, norm_w, w1)


def _rms(x, w):
    return x * lax.rsqrt(jnp.mean(x * x, axis=-1, keepdims=True) + EPS) * w


def _attn_kernel(*refs, seq, n_cache, rope):
    if rope:
        (pq_ref, pkv_ref, ga_ref, cache_ref, tq_ref, tk_ref, qnw_ref, wqb_ref, kvnw_ref, wkvb_ref,
         y_ref, kvup_scr, kpe_scr) = refs
    else:
        (pq_ref, pkv_ref, ga_ref, qnw_ref, wqb_ref, kvnw_ref, wkvb_ref,
         y_ref, newkv_ref, kvup_scr, kpe_scr) = refs

    @pl.when(pl.program_id(1) == 0)
    def _():
        c_kv = _rms(pkv_ref[:, 0:KV_LORA], kvnw_ref[...])
        kpe2 = pkv_ref[:, KV_LORA:KV_LORA + V7X_LANES]
        if rope:
            z = kpe2 * tk_ref[...]
            kpe_own = z + pltpu.roll(z, QK_ROPE, axis=1)
            kvup_scr[0:n_cache, :] = _dot(cache_ref[0, :, 0:KV_LORA].astype(BF16), wkvb_ref[...]).astype(BF16)
            kpe_scr[0:n_cache, :] = cache_ref[0, :, KV_LORA:KV_LORA + V7X_LANES].astype(BF16)
        else:
            lane = lax.broadcasted_iota(jnp.int32, kpe2.shape, 1)
            kpe_own = jnp.where(lane < QK_ROPE, kpe2, 0.0)
            newkv_ref[0, :, 0:KV_LORA] = c_kv
            newkv_ref[0, :, KV_LORA:KV_LORA + QK_ROPE] = kpe2[:, 0:QK_ROPE]
        kvup_scr[n_cache:n_cache + seq, :] = _dot(c_kv.astype(BF16), wkvb_ref[...]).astype(BF16)
        kpe_scr[n_cache:n_cache + seq, :] = kpe_own.astype(BF16)

    qn = _rms(pq_ref[...], qnw_ref[...]).astype(BF16)
    q = _dot(qn, wqb_ref[...])
    sm_scale = float(QK_NOPE + QK_ROPE) ** -0.5
    kpe = kpe_scr[...]
    for h in range(N_HEADS):
        c0 = h * HEAD_QW
        q_nope = q[:, c0:c0 + QK_NOPE]
        q_pe = q[:, c0 + QK_NOPE:c0 + HEAD_QW]
        if rope:
            q_pe = q_pe * tq_ref[...]
        qh = jnp.concatenate([q_nope, q_pe], axis=1).astype(BF16)
        kh = jnp.concatenate([kvup_scr[:, c0:c0 + QK_NOPE], kpe], axis=1)
        s = lax.dot_general(qh, kh, (((1,), (1,)), ((), ())), preferred_element_type=F32) * sm_scale
        s = s - jnp.max(s, axis=-1, keepdims=True)
        e = jnp.exp(s)
        p = (e * (1.0 / jnp.sum(e, axis=-1, keepdims=True))).astype(BF16)
        o = _dot(p, kvup_scr[:, c0 + QK_NOPE:c0 + HEAD_QW])
        g = ga_ref[:, h * V_HEAD:(h + 1) * V_HEAD]
        y_ref[:, h * V_HEAD:(h + 1) * V_HEAD] = (o * _silu(g)).astype(BF16)


def _attention(p, n_seq, seq, lw, cache=None, rope_tab=None):
    m = n_seq * seq
    nqb = seq // ATT_QB
    rope = cache is not None
    n_cache = cache.shape[1] if rope else 0
    half = BRANCH_W // 2
    in_specs = [
        pl.BlockSpec((ATT_QB, Q_LORA), lambda b, q: (b * nqb + q, 0)),
        pl.BlockSpec((seq, half), lambda b, q: (b, 1)),
        pl.BlockSpec((ATT_QB, BRANCH_W), lambda b, q: (b * nqb + q, 1)),
    ]
    args = [p, p, p]
    if rope:
        in_specs += [
            pl.BlockSpec((1, n_cache, KV_LORA + V7X_LANES), lambda b, q: (b, 0, 0)),
            pl.BlockSpec((ATT_QB, V7X_LANES), lambda b, q: (q, 0)),
            pl.BlockSpec((seq, V7X_LANES), lambda b, q: (0, 0)),
        ]
        args += [cache, rope_tab, rope_tab]
    in_specs += [
        pl.BlockSpec((1, Q_LORA), lambda b, q: (0, 0)),
        pl.BlockSpec((Q_LORA, N_HEADS * HEAD_QW), lambda b, q: (0, 0)),
        pl.BlockSpec((1, KV_LORA), lambda b, q: (0, 0)),
        pl.BlockSpec((KV_LORA, N_HEADS * HEAD_QW), lambda b, q: (0, 0)),
    ]
    args += [lw["q_norm_w"], lw["wqb"], lw["kv_norm_w"], lw["wkvb"]]
    out_specs = [pl.BlockSpec((ATT_QB, BRANCH_W), lambda b, q: (b * nqb + q, 0))]
    out_shape = [jax.ShapeDtypeStruct((m, BRANCH_W), BF16)]
    if not rope:
        out_specs.append(pl.BlockSpec((1, seq, KV_LORA + QK_ROPE), lambda b, q: (b, 0, 0)))
        out_shape.append(jax.ShapeDtypeStruct((n_seq, seq, KV_LORA + QK_ROPE), F32))
    lk = n_cache + seq
    return pl.pallas_call(
        functools.partial(_attn_kernel, seq=seq, n_cache=n_cache, rope=rope),
        grid=(n_seq, nqb),
        in_specs=in_specs,
        out_specs=out_specs,
        out_shape=out_shape,
        scratch_shapes=[pltpu.VMEM((lk, N_HEADS * HEAD_QW), BF16), pltpu.VMEM((lk, V7X_LANES), BF16)],
        compiler_params=_cparams(("arbitrary", "arbitrary")),
        name="mla_attention",
    )(*args)


def _dwconv_rows(pad_ref, w_ref, b_ref, seq, taps, first_row, emit):
    for r0 in range(0, seq, ROW_CHUNK):
        acc = pad_ref[first_row + r0:first_row + r0 + ROW_CHUNK, :] * w_ref[0:1, :]
        for k in range(1, taps):
            acc = acc + pad_ref[first_row + r0 + k:first_row + r0 + k + ROW_CHUNK, :] * w_ref[k:k + 1, :]
        emit(r0, acc + b_ref[...])


def _sconv_kernel(bg_ref, cg_ref, xc_ref, gb_ref, w_ref, b_ref, y_ref, pad_scr, *, seq):
    zeros = jnp.zeros((PAD3, CH_TILE), F32)
    pad_scr[0:PAD3, :] = zeros
    pad_scr[PAD3 + seq:PAD3 + seq + PAD3, :] = zeros
    pad_scr[PAD3:PAD3 + seq, :] = cg_ref[...] * xc_ref[...]

    def emit(r0, conv):
        rows = slice(r0, r0 + ROW_CHUNK)
        y_ref[rows, :] = (bg_ref[rows, :] * conv * _silu(gb_ref[rows, :])).astype(BF16)

    _dwconv_rows(pad_scr, w_ref, b_ref, seq, SHORT_K, PAD3 - (SHORT_K - 1) // 2, emit)


def _short_conv(p, n_seq, seq, lw):
    m = n_seq * seq
    ct = BRANCH_W // CH_TILE

    def chunk(c):
        return pl.BlockSpec((seq, CH_TILE), lambda b, j: (b, c * ct + j))

    return pl.pallas_call(
        functools.partial(_sconv_kernel, seq=seq),
        grid=(n_seq, ct),
        in_specs=[chunk(2), chunk(3), chunk(4), chunk(5),
                  pl.BlockSpec((SHORT_K, CH_TILE), lambda b, j: (0, j)),
                  pl.BlockSpec((1, CH_TILE), lambda b, j: (0, j))],
        out_specs=pl.BlockSpec((seq, CH_TILE), lambda b, j: (b, j)),
        out_shape=jax.ShapeDtypeStruct((m, BRANCH_W), BF16),
        scratch_shapes=[pltpu.VMEM((seq + 2 * PAD3, CH_TILE), F32)],
        compiler_params=_cparams(("arbitrary", "arbitrary")),
        name="short_conv",
    )(p, p, p, p, lw["conv3_w"], lw["conv3_b"])


def _pool_kernel(xp_ref, gc_ref, w_ref, sc_ref, y_ref, pad_scr, pooled_scr, *, seq):
    zeros = jnp.zeros((PADP, BRANCH_W), F32)
    pad_scr[0:PADP, :] = zeros
    pad_scr[PADP + seq:PADP + seq + PADP, :] = zeros
    pad_scr[PADP:PADP + seq, :] = xp_ref[...]
    for g, win in enumerate(POOL_WINDOWS):
        cols = slice(g * POOL_GW, (g + 1) * POOL_GW)
        hw = win // 2
        for r0 in range(0, seq, ROW_CHUNK):
            base = PADP - hw + r0
            tot = pad_scr[base:base + ROW_CHUNK, cols]
            for d in range(1, win):
                tot = tot + pad_scr[base + d:base + d + ROW_CHUNK, cols]
            t = r0 + lax.broadcasted_iota(jnp.int32, (ROW_CHUNK, POOL_GW), 0)
            cnt = (jnp.minimum(t + hw, seq) - jnp.maximum(t - hw, 0)).astype(F32)
            pooled = tot / cnt - xp_ref[r0:r0 + ROW_CHUNK, cols]
            pooled_scr[r0:r0 + ROW_CHUNK, cols] = pooled.astype(BF16)
        y = _dot(pooled_scr[:, cols], w_ref[g]) * sc_ref[:, cols] * _silu(gc_ref[:, cols])
        y_ref[:, cols] = y.astype(BF16)


def _pooling(p, n_seq, seq, lw):
    m = n_seq * seq
    return pl.pallas_call(
        functools.partial(_pool_kernel, seq=seq),
        grid=(n_seq,),
        in_specs=[pl.BlockSpec((seq, BRANCH_W), lambda b: (b, 6)),
                  pl.BlockSpec((seq, BRANCH_W), lambda b: (b, 7)),
                  pl.BlockSpec((len(POOL_WINDOWS), POOL_GW, POOL_GW), lambda b: (0, 0, 0)),
                  pl.BlockSpec((1, BRANCH_W), lambda b: (0, 0))],
        out_specs=pl.BlockSpec((seq, BRANCH_W), lambda b: (b, 0)),
        out_shape=jax.ShapeDtypeStruct((m, BRANCH_W), BF16),
        scratch_shapes=[pltpu.VMEM((seq + 2 * PADP, BRANCH_W), F32),
                        pltpu.VMEM((seq, BRANCH_W), BF16)],
        compiler_params=_cparams(("arbitrary",)),
        name="ms_pool",
    )(p, p, lw["pool_w"], lw["pool_scale"])


def _cconv_kernel(ga_ref, gb_ref, gd_ref, w_ref, b_ref, lnw_ref, lnb_ref, y_ref, pad_scr, conv_scr, *, seq):
    j = pl.program_id(1)
    zeros = jnp.zeros((PAD31, CH_TILE), F32)
    pad_scr[0:PAD31, :] = zeros
    pad_scr[PAD31 + seq:PAD31 + seq + PAD31, :] = zeros
    pad_scr[PAD31:PAD31 + seq, :] = ga_ref[...] * jax.nn.sigmoid(gb_ref[...])

    def emit(r0, conv):
        conv_scr[j, r0:r0 + ROW_CHUNK, :] = conv

    _dwconv_rows(pad_scr, w_ref, b_ref, seq, CONF_K, PAD31 - (CONF_K - 1) // 2, emit)

    ct = BRANCH_W // CH_TILE

    @pl.when(j == ct - 1)
    def _():
        for r0 in range(0, seq, ROW_CHUNK):
            rows = slice(r0, r0 + ROW_CHUNK)
            u = jnp.concatenate([conv_scr[c, rows, :] for c in range(ct)], axis=1)
            mu = jnp.mean(u, axis=-1, keepdims=True)
            var = jnp.mean(jnp.square(u - mu), axis=-1, keepdims=True)
            v = (u - mu) * lax.rsqrt(var + EPS) * lnw_ref[...] + lnb_ref[...]
            y_ref[rows, :] = (_silu(v) * _silu(gd_ref[rows, :])).astype(BF16)


def _conformer_conv(p, n_seq, seq, lw):
    m = n_seq * seq
    ct = BRANCH_W // CH_TILE
    return pl.pallas_call(
        functools.partial(_cconv_kernel, seq=seq),
        grid=(n_seq, ct),
        in_specs=[pl.BlockSpec((seq, CH_TILE), lambda b, j: (b, 8 * ct + j)),
                  pl.BlockSpec((seq, CH_TILE), lambda b, j: (b, 9 * ct + j)),
                  pl.BlockSpec((seq, BRANCH_W), lambda b, j: (b, 10)),
                  pl.BlockSpec((CONF_K, CH_TILE), lambda b, j: (0, j)),
                  pl.BlockSpec((1, CH_TILE), lambda b, j: (0, j)),
                  pl.BlockSpec((1, BRANCH_W), lambda b, j: (0, 0)),
                  pl.BlockSpec((1, BRANCH_W), lambda b, j: (0, 0))],
        out_specs=pl.BlockSpec((seq, BRANCH_W), lambda b, j: (b, 0)),
        out_shape=jax.ShapeDtypeStruct((m, BRANCH_W), BF16),
        scratch_shapes=[pltpu.VMEM((seq + 2 * PAD31, CH_TILE), F32),
                        pltpu.VMEM((ct, seq, CH_TILE), F32)],
        compiler_params=_cparams(("arbitrary", "arbitrary")),
        name="conformer_conv",
    )(p, p, p, lw["dw_w"], lw["dw_b"], lw["cln_w"], lw["cln_b"])


MERGE_TN = 512
MERGE_NT = D_MODEL // MERGE_TN


def _merge_kernel(h_ref, ya_ref, yb_ref, yc_ref, yd_ref, wml_ref, wb_ref, wo_ref, x_ref, mod_ref,
                  o_ref, merged_scr):
    n = pl.program_id(1)

    @pl.when(n < MERGE_NT)
    def _():
        h = h_ref[...]
        acc = None
        for i, y_ref in enumerate((ya_ref, yb_ref, yc_ref, yd_ref)):
            gate = jax.nn.sigmoid(_dot(h, wml_ref[i]))
            term = gate * _dot(y_ref[...], wb_ref[i])
            acc = term if acc is None else acc + term
        merged_scr[n] = acc.astype(BF16)

    @pl.when(n >= MERGE_NT)
    def _():
        out = _dot(merged_scr[0], wo_ref[0:MERGE_TN, :])
        for k in range(1, MERGE_NT):
            out = out + _dot(merged_scr[k], wo_ref[k * MERGE_TN:(k + 1) * MERGE_TN, :])
        o_ref[...] = x_ref[...] + mod_ref[0] * out


def _merge(h, ys, lw, x, gate3, rows_per_cond, cond_row0, tm):
    m = x.shape[0]
    tiles_per_cond = rows_per_cond // tm
    last = MERGE_NT - 1

    def first_phase(n):
        return jnp.minimum(n, last)

    def second_phase(n):
        return jnp.maximum(n - MERGE_NT, 0)

    y_spec = pl.BlockSpec((tm, BRANCH_W), lambda i, n: (i, 0))
    return pl.pallas_call(
        _merge_kernel,
        grid=(m // tm, 2 * MERGE_NT),
        in_specs=[
            pl.BlockSpec((tm, D_MODEL), lambda i, n: (i, 0)),
            y_spec, y_spec, y_spec, y_spec,
            pl.BlockSpec((N_BRANCH, D_MODEL, MERGE_TN), lambda i, n: (0, 0, first_phase(n))),
            pl.BlockSpec((N_BRANCH, BRANCH_W, MERGE_TN), lambda i, n: (0, 0, first_phase(n))),
            pl.BlockSpec((D_MODEL, MERGE_TN), lambda i, n: (0, second_phase(n))),
            pl.BlockSpec((tm, MERGE_TN), lambda i, n: (i, second_phase(n))),
            pl.BlockSpec((1, 1, MERGE_TN), lambda i, n: (cond_row0 + i // tiles_per_cond, 0, second_phase(n))),
        ],
        out_specs=pl.BlockSpec((tm, MERGE_TN), lambda i, n: (i, second_phase(n))),
        out_shape=jax.ShapeDtypeStruct((m, D_MODEL), F32),
        scratch_shapes=[pltpu.VMEM((MERGE_NT, tm, MERGE_TN), BF16)],
        compiler_params=_cparams(("arbitrary", "arbitrary")),
        name="merge_outproj",
    )(h, *ys, lw["wml"], lw["wb"], lw["wo"], x, gate3)


def _final_norm_kernel(x_ref, w_ref, o_ref):
    o_ref[...] = _rms(x_ref[...], w_ref[...])


def _final_norm(x, w, tm=512):
    m = x.shape[0]
    return pl.pallas_call(
        _final_norm_kernel,
        grid=(m // tm,),
        in_specs=[pl.BlockSpec((tm, D_MODEL), lambda i: (i, 0)),
                  pl.BlockSpec((1, D_MODEL), lambda i: (0, 0))],
        out_specs=pl.BlockSpec((tm, D_MODEL), lambda i: (i, 0)),
        out_shape=jax.ShapeDtypeStruct((m, D_MODEL), F32),
        compiler_params=_cparams(("arbitrary",)),
        name="final_norm",
    )(x, w)


def _rope_swap_cols(w):
    q = QK_ROPE // 4
    return jnp.concatenate([w[..., q:2 * q], w[..., 0:q], w[..., 3 * q:4 * q], w[..., 2 * q:3 * q]], axis=-1)


def _pack_layer(l, w_in, q_norm_w, w_qb, kv_norm_w, w_kvb, conv3_w, conv3_b, pool_w, pool_scale,
                dw_w, dw_b, cln_w, cln_b, w_bproj, w_out):
    wi = w_in[l]
    a_cols = Q_LORA + KV_LORA + QK_ROPE
    rest_end = a_cols + 10 * BRANCH_W
    k_pe_w = wi[:, Q_LORA + KV_LORA:a_cols]
    w1 = jnp.concatenate([
        wi[:, :a_cols], _rope_swap_cols(k_pe_w),
        jnp.zeros((D_MODEL, BRANCH_W - a_cols - QK_ROPE), F32), wi[:, a_cols:rest_end]], axis=1).astype(BF16)
    wml = wi[:, rest_end:].reshape(D_MODEL, N_BRANCH, D_MODEL).transpose(1, 0, 2).astype(BF16)
    wq = w_qb[l].reshape(Q_LORA, N_HEADS, QK_NOPE + QK_ROPE)
    wq_pe = wq[..., QK_NOPE:]
    wqb = jnp.concatenate([wq, _rope_swap_cols(wq_pe)], axis=-1).reshape(Q_LORA, N_HEADS * HEAD_QW).astype(BF16)
    return {
        "w1": w1, "wml": wml, "wqb": wqb,
        "wkvb": w_kvb[l].astype(BF16),
        "q_norm_w": q_norm_w[l].reshape(1, Q_LORA), "kv_norm_w": kv_norm_w[l].reshape(1, KV_LORA),
        "conv3_w": conv3_w[l], "conv3_b": conv3_b[l].reshape(1, BRANCH_W),
        "pool_w": pool_w[l].astype(BF16), "pool_scale": pool_scale[l].reshape(1, BRANCH_W),
        "dw_w": dw_w[l], "dw_b": dw_b[l].reshape(1, BRANCH_W),
        "cln_w": cln_w[l].reshape(1, BRANCH_W), "cln_b": cln_b[l].reshape(1, BRANCH_W),
        "wb": w_bproj[l].astype(BF16), "wo": w_out[l].astype(BF16),
    }


def _rope_table(n_tokens):
    t = np.arange(n_tokens)
    half = QK_ROPE // 2
    inv = ROPE_BASE ** (-np.arange(0, half, 2, dtype=np.float32) / half)
    ang_r = (t // GRID_W).astype(np.float32)[:, None] * inv
    ang_c = (t % GRID_W).astype(np.float32)[:, None] * inv
    cos = np.concatenate([np.cos(ang_r), np.cos(ang_r), np.cos(ang_c), np.cos(ang_c)], axis=1)
    sin = np.concatenate([-np.sin(ang_r), np.sin(ang_r), -np.sin(ang_c), np.sin(ang_c)], axis=1)
    return jnp.asarray(np.concatenate([cos, sin], axis=1), dtype=F32)


def _stream(x, n_seq, seq, mods, layers, cond_row0, rows_per_cond, final_w, tm, caches=None, rope_tab=None):
    new_kvs = []
    for l in range(DEPTH):
        lw = layers[l]
        mod3 = mods[l].reshape(8, 1, 3 * D_MODEL)
        gate3 = mods[l][:, 2 * D_MODEL:].reshape(8, 1, D_MODEL)
        h, p = _inproj(x, mod3, lw["norm_w"], lw["w1"], rows_per_cond, cond_row0, tm)
        if caches is None:
            y_a, new_kv = _attention(p, n_seq, seq, lw)
            new_kvs.append(new_kv)
        else:
            (y_a,) = _attention(p, n_seq, seq, lw, cache=caches[l], rope_tab=rope_tab)
        y_b = _short_conv(p, n_seq, seq, lw)
        y_c = _pooling(p, n_seq, seq, lw)
        y_d = _conformer_conv(p, n_seq, seq, lw)
        x = _merge(h, (y_a, y_b, y_c, y_d), lw, x, gate3, rows_per_cond, cond_row0, tm)
    return _final_norm(x, final_w), new_kvs


def kernel(x_prompt, x_sample, cache_kv, c, c_ctx, w_ada, b_ada, norm_w, w_in, q_norm_w, w_qb, kv_norm_w,
           w_kvb, conv3_w, conv3_b, pool_w, pool_scale, dw_w, dw_b, cln_w, cln_b, w_bproj, w_out,
           final_norm_w):
    batch, seq, _ = x_prompt.shape
    dec_batch, dec_seq, _ = x_sample.shape
    assert 1 + dec_batch <= 8

    cond = jnp.concatenate([c_ctx[None, :], c, jnp.zeros((8 - 1 - dec_batch, D_MODEL), F32)], axis=0)
    mods = _modulation(cond, w_ada, b_ada)

    layers = []
    for l in range(DEPTH):
        lw = _pack_layer(l, w_in, q_norm_w, w_qb, kv_norm_w, w_kvb, conv3_w, conv3_b, pool_w, pool_scale,
                         dw_w, dw_b, cln_w, cln_b, w_bproj, w_out)
        lw["norm_w"] = norm_w[l].reshape(1, D_MODEL)
        layers.append(lw)
    final_w = final_norm_w.reshape(1, D_MODEL)

    tm = 512
    xp = x_prompt.reshape(batch * seq, D_MODEL)
    y_p, new_kvs = _stream(xp, batch, seq, mods, layers, 0, batch * seq, final_w, tm)

    caches = [jnp.concatenate([cache_kv[:, l], cache_kv[:, l, :, KV_LORA:]], axis=-1) for l in range(DEPTH)]
    xs = x_sample.reshape(dec_batch * dec_seq, D_MODEL)
    y_s, _ = _stream(xs, dec_batch, dec_seq, mods, layers, 1, dec_seq, final_w, tm,
                     caches=caches, rope_tab=_rope_table(dec_seq))

    return (y_p.reshape(batch, seq, D_MODEL), y_s.reshape(dec_batch, dec_seq, D_MODEL),
            jnp.stack(new_kvs, axis=1))
```

```python
import functools

import numpy as np
import jax
import jax.numpy as jnp
from jax import lax
from jax.experimental import pallas as pl
from jax.experimental.pallas import tpu as pltpu

F32 = jnp.float32
BF16 = jnp.bfloat16

D_MODEL = 2048
DEPTH = 2
GRID_W = 64
N_BRANCH = 4
BRANCH_W = D_MODEL // 2
QK_NOPE = 128
QK_ROPE = 64
V_HEAD = 128
N_HEADS = BRANCH_W // V_HEAD
Q_LORA = 512
KV_LORA = 256
ROPE_BASE = 10000.0
SHORT_K = 3
POOL_WINDOWS = (2, 4, 8, 16)
POOL_GW = BRANCH_W // len(POOL_WINDOWS)
CONF_K = 31
EPS = 1e-6

V7X_LANES = 128
V7X_SUBLANES = 8
V7X_VMEM_BYTES = 64 * 1024 * 1024
VMEM_LIMIT = V7X_VMEM_BYTES - 8 * 1024 * 1024

P_CHUNKS = 10
A_COLS = Q_LORA + KV_LORA + QK_ROPE
KV_W = 512
HEAD_QW = 256
ATT_QB = 256
CH_TILE = 256
ROW_CHUNK = 64
PAD3 = V7X_SUBLANES
PAD31 = 2 * V7X_SUBLANES
PADP = V7X_SUBLANES
INPROJ_TN = 2 * BRANCH_W
MERGE_TN = 512
MERGE_NT = D_MODEL // MERGE_TN


def _cparams(sem):
    return pltpu.CompilerParams(dimension_semantics=sem, vmem_limit_bytes=VMEM_LIMIT)


def _silu(x):
    return x * jax.nn.sigmoid(x)


def _dot(a, b):
    return jnp.dot(a, b, preferred_element_type=F32)


def _rms(x, w):
    return x * lax.rsqrt(jnp.mean(x * x, axis=-1, keepdims=True) + EPS) * w


def _modulate(x, nw, mod_row):
    shift = mod_row[:, 0:D_MODEL]
    scale = mod_row[:, D_MODEL:2 * D_MODEL]
    return _rms(x, nw) * (1.0 + scale) + shift


def _mod_kernel(c_ref, w_ref, b_ref, o_ref):
    s = _silu(c_ref[...]).astype(BF16)
    o_ref[0] = _dot(s, w_ref[0].astype(BF16)) + b_ref[0]


def _modulation(cond, w_ada, b_ada):
    tn = 768
    nt = (3 * D_MODEL) // tn
    return pl.pallas_call(
        _mod_kernel,
        grid=(DEPTH, nt),
        in_specs=[
            pl.BlockSpec((8, D_MODEL), lambda l, n: (0, 0)),
            pl.BlockSpec((1, D_MODEL, tn), lambda l, n: (l, 0, n)),
            pl.BlockSpec((1, 1, tn), lambda l, n: (l, 0, n)),
        ],
        out_specs=pl.BlockSpec((1, 8, tn), lambda l, n: (l, 0, n)),
        out_shape=jax.ShapeDtypeStruct((DEPTH, 8, 3 * D_MODEL), F32),
        compiler_params=_cparams(("arbitrary", "arbitrary")),
        name="adaln_mod",
    )(cond, w_ada, b_ada.reshape(DEPTH, 1, 3 * D_MODEL))


def _hnorm_kernel(x_ref, mod_ref, nw_ref, h_ref):
    h_ref[...] = _modulate(x_ref[...], nw_ref[...], mod_ref[0]).astype(BF16)


def _hnorm(x, mod3, norm_w, cond_map, tm):
    m = x.shape[0]
    return pl.pallas_call(
        _hnorm_kernel,
        grid=(m // tm,),
        in_specs=[pl.BlockSpec((tm, D_MODEL), lambda i: (i, 0)),
                  pl.BlockSpec((1, 1, 3 * D_MODEL), lambda i: (cond_map(i, tm), 0, 0)),
                  pl.BlockSpec((1, D_MODEL), lambda i: (0, 0))],
        out_specs=pl.BlockSpec((tm, D_MODEL), lambda i: (i, 0)),
        out_shape=jax.ShapeDtypeStruct((m, D_MODEL), BF16),
        compiler_params=_cparams(("arbitrary",)),
        name="mod_norm",
    )(x, mod3, norm_w)


def _inproj_kernel(h_ref, w_ref, p_ref):
    p_ref[...] = _dot(h_ref[...], w_ref[...])


def _inproj(h, wr, tm):
    m = h.shape[0]
    return pl.pallas_call(
        _inproj_kernel,
        grid=(P_CHUNKS * BRANCH_W // INPROJ_TN, m // tm),
        in_specs=[pl.BlockSpec((tm, D_MODEL), lambda n, i: (i, 0)),
                  pl.BlockSpec((D_MODEL, INPROJ_TN), lambda n, i: (0, n))],
        out_specs=pl.BlockSpec((tm, INPROJ_TN), lambda n, i: (i, n)),
        out_shape=jax.ShapeDtypeStruct((m, P_CHUNKS * BRANCH_W), F32),
        compiler_params=_cparams(("arbitrary", "arbitrary")),
        name="inproj",
    )(h, wr)


def _attn_kernel(*refs, seq, n_cache, rope):
    if rope:
        (hq_ref, hkv_ref, ga_ref, cache_ref, tq_ref, tk_ref, waq_ref, wakv_ref, qnw_ref, wqb_ref, kvnw_ref,
         wkvb_ref, y_ref, kvup_scr, kpe_scr) = refs
    else:
        (hq_ref, hkv_ref, ga_ref, waq_ref, wakv_ref, qnw_ref, wqb_ref, kvnw_ref, wkvb_ref,
         y_ref, newkv_ref, kvup_scr, kpe_scr) = refs

    @pl.when(pl.program_id(1) == 0)
    def _():
        pkv = _dot(hkv_ref[...], wakv_ref[...])
        c_kv = _rms(pkv[:, 0:KV_LORA], kvnw_ref[...])
        kpe2 = pkv[:, KV_LORA:KV_LORA + V7X_LANES]
        if rope:
            z = kpe2 * tk_ref[...]
            kpe_own = z + pltpu.roll(z, QK_ROPE, axis=1)
            kvup_scr[0:n_cache, :] = _dot(cache_ref[0, :, 0:KV_LORA].astype(BF16), wkvb_ref[...]).astype(BF16)
            kpe_scr[0:n_cache, :] = cache_ref[0, :, KV_LORA:KV_LORA + V7X_LANES].astype(BF16)
        else:
            lane = lax.broadcasted_iota(jnp.int32, kpe2.shape, 1)
            kpe_own = jnp.where(lane < QK_ROPE, kpe2, 0.0)
            newkv_ref[0, :, 0:KV_LORA] = c_kv
            newkv_ref[0, :, KV_LORA:KV_LORA + QK_ROPE] = kpe2[:, 0:QK_ROPE]
        kvup_scr[n_cache:n_cache + seq, :] = _dot(c_kv.astype(BF16), wkvb_ref[...]).astype(BF16)
        kpe_scr[n_cache:n_cache + seq, :] = kpe_own.astype(BF16)

    qa = _dot(hq_ref[...], waq_ref[...])
    qn = _rms(qa, qnw_ref[...]).astype(BF16)
    q = _dot(qn, wqb_ref[...])
    sm_scale = float(QK_NOPE + QK_ROPE) ** -0.5
    kpe = kpe_scr[...]
    for h in range(N_HEADS):
        c0 = h * HEAD_QW
        q_nope = q[:, c0:c0 + QK_NOPE]
        q_pe = q[:, c0 + QK_NOPE:c0 + HEAD_QW]
        if rope:
            q_pe = q_pe * tq_ref[...]
        qh = jnp.concatenate([q_nope, q_pe], axis=1).astype(BF16)
        kh = jnp.concatenate([kvup_scr[:, c0:c0 + QK_NOPE], kpe], axis=1)
        s = lax.dot_general(qh, kh, (((1,), (1,)), ((), ())), preferred_element_type=F32) * sm_scale
        s = s - jnp.max(s, axis=-1, keepdims=True)
        e = jnp.exp(s)
        p = (e * (1.0 / jnp.sum(e, axis=-1, keepdims=True))).astype(BF16)
        o = _dot(p, kvup_scr[:, c0 + QK_NOPE:c0 + HEAD_QW])
        g = ga_ref[:, h * V_HEAD:(h + 1) * V_HEAD]
        y_ref[:, h * V_HEAD:(h + 1) * V_HEAD] = (o * _silu(g)).astype(BF16)


def _attention(h, p, n_seq, seq, lw, cache=None, rope_tab=None):
    m = n_seq * seq
    nqb = seq // ATT_QB
    rope = cache is not None
    n_cache = cache.shape[1] if rope else 0
    in_specs = [
        pl.BlockSpec((ATT_QB, D_MODEL), lambda b, q: (b * nqb + q, 0)),
        pl.BlockSpec((seq, D_MODEL), lambda b, q: (b, 0)),
        pl.BlockSpec((ATT_QB, BRANCH_W), lambda b, q: (b * nqb + q, 0)),
    ]
    args = [h, h, p]
    if rope:
        in_specs += [
            pl.BlockSpec((1, n_cache, KV_LORA + V7X_LANES), lambda b, q: (b, 0, 0)),
            pl.BlockSpec((ATT_QB, V7X_LANES), lambda b, q: (q, 0)),
            pl.BlockSpec((seq, V7X_LANES), lambda b, q: (0, 0)),
        ]
        args += [cache, rope_tab, rope_tab]
    in_specs += [
        pl.BlockSpec((D_MODEL, Q_LORA), lambda b, q: (0, 0)),
        pl.BlockSpec((D_MODEL, KV_W), lambda b, q: (0, 0)),
        pl.BlockSpec((1, Q_LORA), lambda b, q: (0, 0)),
        pl.BlockSpec((Q_LORA, N_HEADS * HEAD_QW), lambda b, q: (0, 0)),
        pl.BlockSpec((1, KV_LORA), lambda b, q: (0, 0)),
        pl.BlockSpec((KV_LORA, N_HEADS * HEAD_QW), lambda b, q: (0, 0)),
    ]
    args += [lw["wa_q"], lw["wa_kv"], lw["q_norm_w"], lw["wqb"], lw["kv_norm_w"], lw["wkvb"]]
    out_specs = [pl.BlockSpec((ATT_QB, BRANCH_W), lambda b, q: (b * nqb + q, 0))]
    out_shape = [jax.ShapeDtypeStruct((m, BRANCH_W), BF16)]
    if not rope:
        out_specs.append(pl.BlockSpec((1, seq, KV_LORA + QK_ROPE), lambda b, q: (b, 0, 0)))
        out_shape.append(jax.ShapeDtypeStruct((n_seq, seq, KV_LORA + QK_ROPE), F32))
    lk = n_cache + seq
    return pl.pallas_call(
        functools.partial(_attn_kernel, seq=seq, n_cache=n_cache, rope=rope),
        grid=(n_seq, nqb),
        in_specs=in_specs,
        out_specs=out_specs,
        out_shape=out_shape,
        scratch_shapes=[pltpu.VMEM((lk, N_HEADS * HEAD_QW), BF16), pltpu.VMEM((lk, V7X_LANES), BF16)],
        compiler_params=_cparams(("arbitrary", "arbitrary")),
        name="mla_attention",
    )(*args)


def _dwconv_rows(pad_ref, w_ref, b_ref, cols, seq, taps, first_row, emit):
    for r0 in range(0, seq, ROW_CHUNK):
        acc = pad_ref[first_row + r0:first_row + r0 + ROW_CHUNK, :] * w_ref[0:1, cols]
        for k in range(1, taps):
            acc = acc + pad_ref[first_row + r0 + k:first_row + r0 + k + ROW_CHUNK, :] * w_ref[k:k + 1, cols]
        emit(r0, acc + b_ref[:, cols])


def _sconv_kernel(bg_ref, cg_ref, xc_ref, gb_ref, w_ref, b_ref, y_ref, pad_scr, *, seq, chb):
    zeros = jnp.zeros((PAD3, CH_TILE), F32)
    pad_scr[0:PAD3, :] = zeros
    pad_scr[PAD3 + seq:PAD3 + seq + PAD3, :] = zeros
    for c0 in range(0, chb, CH_TILE):
        cols = slice(c0, c0 + CH_TILE)
        pad_scr[PAD3:PAD3 + seq, :] = cg_ref[:, cols] * xc_ref[:, cols]

        def emit(r0, conv, cols=cols):
            rows = slice(r0, r0 + ROW_CHUNK)
            y_ref[rows, cols] = (bg_ref[rows, cols] * conv * _silu(gb_ref[rows, cols])).astype(BF16)

        _dwconv_rows(pad_scr, w_ref, b_ref, cols, seq, SHORT_K, PAD3 - (SHORT_K - 1) // 2, emit)


def _short_conv(p, n_seq, seq, chb, lw):
    m = n_seq * seq
    ct = BRANCH_W // chb

    def chunk(c):
        return pl.BlockSpec((seq, chb), lambda b, j: (b, c * ct + j))

    return pl.pallas_call(
        functools.partial(_sconv_kernel, seq=seq, chb=chb),
        grid=(n_seq, ct),
        in_specs=[chunk(1), chunk(2), chunk(3), chunk(4),
                  pl.BlockSpec((SHORT_K, chb), lambda b, j: (0, j)),
                  pl.BlockSpec((1, chb), lambda b, j: (0, j))],
        out_specs=pl.BlockSpec((seq, chb), lambda b, j: (b, j)),
        out_shape=jax.ShapeDtypeStruct((m, BRANCH_W), BF16),
        scratch_shapes=[pltpu.VMEM((seq + 2 * PAD3, CH_TILE), F32)],
        compiler_params=_cparams(("arbitrary", "arbitrary")),
        name="short_conv",
    )(p, p, p, p, lw["conv3_w"], lw["conv3_b"])


def _pool_kernel(xp_ref, gc_ref, w_ref, sc_ref, y_ref, pad_scr, pooled_scr, *, seq):
    zeros = jnp.zeros((PADP, BRANCH_W), F32)
    pad_scr[0:PADP, :] = zeros
    pad_scr[PADP + seq:PADP + seq + PADP, :] = zeros
    pad_scr[PADP:PADP + seq, :] = xp_ref[...]
    for g, win in enumerate(POOL_WINDOWS):
        cols = slice(g * POOL_GW, (g + 1) * POOL_GW)
        hw = win // 2
        for r0 in range(0, seq, ROW_CHUNK):
            base = PADP - hw + r0
            tot = pad_scr[base:base + ROW_CHUNK, cols]
            for d in range(1, win):
                tot = tot + pad_scr[base + d:base + d + ROW_CHUNK, cols]
            t = r0 + lax.broadcasted_iota(jnp.int32, (ROW_CHUNK, POOL_GW), 0)
            cnt = (jnp.minimum(t + hw, seq) - jnp.maximum(t - hw, 0)).astype(F32)
            pooled = tot / cnt - xp_ref[r0:r0 + ROW_CHUNK, cols]
            pooled_scr[r0:r0 + ROW_CHUNK, cols] = pooled.astype(BF16)
        y = _dot(pooled_scr[:, cols], w_ref[g]) * sc_ref[:, cols] * _silu(gc_ref[:, cols])
        y_ref[:, cols] = y.astype(BF16)


def _pooling(p, n_seq, seq, lw):
    m = n_seq * seq
    return pl.pallas_call(
        functools.partial(_pool_kernel, seq=seq),
        grid=(n_seq,),
        in_specs=[pl.BlockSpec((seq, BRANCH_W), lambda b: (b, 5)),
                  pl.BlockSpec((seq, BRANCH_W), lambda b: (b, 6)),
                  pl.BlockSpec((len(POOL_WINDOWS), POOL_GW, POOL_GW), lambda b: (0, 0, 0)),
                  pl.BlockSpec((1, BRANCH_W), lambda b: (0, 0))],
        out_specs=pl.BlockSpec((seq, BRANCH_W), lambda b: (b, 0)),
        out_shape=jax.ShapeDtypeStruct((m, BRANCH_W), BF16),
        scratch_shapes=[pltpu.VMEM((seq + 2 * PADP, BRANCH_W), F32),
                        pltpu.VMEM((seq, BRANCH_W), BF16)],
        compiler_params=_cparams(("arbitrary",)),
        name="ms_pool",
    )(p, p, lw["pool_w"], lw["pool_scale"])


def _cconv_kernel(ga_ref, gb_ref, gd_ref, w_ref, b_ref, lnw_ref, lnb_ref, y_ref, pad_scr, conv_scr, *, seq, chb):
    j = pl.program_id(1)
    sub = chb // CH_TILE
    zeros = jnp.zeros((PAD31, CH_TILE), F32)
    pad_scr[0:PAD31, :] = zeros
    pad_scr[PAD31 + seq:PAD31 + seq + PAD31, :] = zeros
    for s in range(sub):
        cols = slice(s * CH_TILE, (s + 1) * CH_TILE)
        pad_scr[PAD31:PAD31 + seq, :] = ga_ref[:, cols] * jax.nn.sigmoid(gb_ref[:, cols])

        def emit(r0, conv, s=s):
            conv_scr[j * sub + s, r0:r0 + ROW_CHUNK, :] = conv

        _dwconv_rows(pad_scr, w_ref, b_ref, cols, seq, CONF_K, PAD31 - (CONF_K - 1) // 2, emit)

    n_tiles = BRANCH_W // CH_TILE

    @pl.when(j == BRANCH_W // chb - 1)
    def _():
        for r0 in range(0, seq, ROW_CHUNK):
            rows = slice(r0, r0 + ROW_CHUNK)
            u = jnp.concatenate([conv_scr[c, rows, :] for c in range(n_tiles)], axis=1)
            mu = jnp.mean(u, axis=-1, keepdims=True)
            var = jnp.mean(jnp.square(u - mu), axis=-1, keepdims=True)
            v = (u - mu) * lax.rsqrt(var + EPS) * lnw_ref[...] + lnb_ref[...]
            y_ref[rows, :] = (_silu(v) * _silu(gd_ref[rows, :])).astype(BF16)


def _conformer_conv(p, n_seq, seq, chb, lw):
    m = n_seq * seq
    ct = BRANCH_W // chb
    return pl.pallas_call(
        functools.partial(_cconv_kernel, seq=seq, chb=chb),
        grid=(n_seq, ct),
        in_specs=[pl.BlockSpec((seq, chb), lambda b, j: (b, 7 * ct + j)),
                  pl.BlockSpec((seq, chb), lambda b, j: (b, 8 * ct + j)),
                  pl.BlockSpec((seq, BRANCH_W), lambda b, j: (b, 9)),
                  pl.BlockSpec((CONF_K, chb), lambda b, j: (0, j)),
                  pl.BlockSpec((1, chb), lambda b, j: (0, j)),
                  pl.BlockSpec((1, BRANCH_W), lambda b, j: (0, 0)),
                  pl.BlockSpec((1, BRANCH_W), lambda b, j: (0, 0))],
        out_specs=pl.BlockSpec((seq, BRANCH_W), lambda b, j: (b, 0)),
        out_shape=jax.ShapeDtypeStruct((m, BRANCH_W), BF16),
        scratch_shapes=[pltpu.VMEM((seq + 2 * PAD31, CH_TILE), F32),
                        pltpu.VMEM((BRANCH_W // CH_TILE, seq, CH_TILE), F32)],
        compiler_params=_cparams(("arbitrary", "arbitrary")),
        name="conformer_conv",
    )(p, p, p, lw["dw_w"], lw["dw_b"], lw["cln_w"], lw["cln_b"])


def _merge_kernel(h_ref, ya_ref, yb_ref, yc_ref, yd_ref, wm0_ref, wm1_ref, wm2_ref, wm3_ref, wb_ref, o_ref):
    h = h_ref[...]
    acc = None
    for i, (y_ref, wm_ref) in enumerate(((ya_ref, wm0_ref), (yb_ref, wm1_ref), (yc_ref, wm2_ref),
                                         (yd_ref, wm3_ref))):
        gate = jax.nn.sigmoid(_dot(h, wm_ref[...]))
        term = gate * _dot(y_ref[...], wb_ref[i])
        acc = term if acc is None else acc + term
    o_ref[...] = acc.astype(BF16)


def _merge(h, ys, lw, tm):
    m = h.shape[0]
    y_spec = pl.BlockSpec((tm, BRANCH_W), lambda n, i: (i, 0))

    def wm_spec(branch):
        return pl.BlockSpec((D_MODEL, MERGE_TN), lambda n, i: (0, branch * MERGE_NT + n))

    return pl.pallas_call(
        _merge_kernel,
        grid=(MERGE_NT, m // tm),
        in_specs=[pl.BlockSpec((tm, D_MODEL), lambda n, i: (i, 0)),
                  y_spec, y_spec, y_spec, y_spec,
                  wm_spec(0), wm_spec(1), wm_spec(2), wm_spec(3),
                  pl.BlockSpec((N_BRANCH, BRANCH_W, MERGE_TN), lambda n, i: (0, 0, n))],
        out_specs=pl.BlockSpec((tm, MERGE_TN), lambda n, i: (i, n)),
        out_shape=jax.ShapeDtypeStruct((m, D_MODEL), BF16),
        compiler_params=_cparams(("arbitrary", "arbitrary")),
        name="merge",
    )(h, *ys, lw["wml"], lw["wml"], lw["wml"], lw["wml"], lw["wb"])


def _outproj_kernel(mg_ref, wo_ref, x_ref, mod_ref, nw_ref, *rest, last):
    gate = mod_ref[0, :, 2 * D_MODEL:3 * D_MODEL]
    xn = x_ref[...] + gate * _dot(mg_ref[...], wo_ref[...])
    if last:
        (y_ref,) = rest
        y_ref[...] = _rms(xn, nw_ref[...])
    else:
        modn_ref, xo_ref, h_ref = rest
        xo_ref[...] = xn
        h_ref[...] = _modulate(xn, nw_ref[...], modn_ref[0]).astype(BF16)


def _outproj(merged, wo, x, mod3, nw, cond_map, tm, mod3_next=None):
    m = x.shape[0]
    last = mod3_next is None
    row = pl.BlockSpec((tm, D_MODEL), lambda i: (i, 0))
    mod_spec = pl.BlockSpec((1, 1, 3 * D_MODEL), lambda i: (cond_map(i, tm), 0, 0))
    in_specs = [row, pl.BlockSpec((D_MODEL, D_MODEL), lambda i: (0, 0)), row, mod_spec,
                pl.BlockSpec((1, D_MODEL), lambda i: (0, 0))]
    args = [merged, wo, x, mod3, nw]
    if last:
        out_specs = row
        out_shape = jax.ShapeDtypeStruct((m, D_MODEL), F32)
    else:
        in_specs.append(mod_spec)
        args.append(mod3_next)
        out_specs = [row, row]
        out_shape = [jax.ShapeDtypeStruct((m, D_MODEL), F32), jax.ShapeDtypeStruct((m, D_MODEL), BF16)]
    return pl.pallas_call(
        functools.partial(_outproj_kernel, last=last),
        grid=(m // tm,),
        in_specs=in_specs,
        out_specs=out_specs,
        out_shape=out_shape,
        compiler_params=_cparams(("arbitrary",)),
        name="outproj",
    )(*args)


def _rope_swap_cols(w):
    q = QK_ROPE // 4
    return jnp.concatenate([w[..., q:2 * q], w[..., 0:q], w[..., 3 * q:4 * q], w[..., 2 * q:3 * q]], axis=-1)


def _pack_layer(l, w_in, q_norm_w, w_qb, kv_norm_w, w_kvb, conv3_w, conv3_b, pool_w, pool_scale,
                dw_w, dw_b, cln_w, cln_b, w_bproj, w_out):
    wi = w_in[l]
    rest_end = A_COLS + P_CHUNKS * BRANCH_W
    k_pe_w = wi[:, Q_LORA + KV_LORA:A_COLS]
    wa_kv = jnp.concatenate([wi[:, Q_LORA:A_COLS], _rope_swap_cols(k_pe_w),
                             jnp.zeros((D_MODEL, KV_W - KV_LORA - 2 * QK_ROPE), F32)], axis=1).astype(BF16)
    wq = w_qb[l].reshape(Q_LORA, N_HEADS, QK_NOPE + QK_ROPE)
    wqb = jnp.concatenate([wq, _rope_swap_cols(wq[..., QK_NOPE:])], axis=-1)
    return {
        "wa_q": wi[:, :Q_LORA].astype(BF16), "wa_kv": wa_kv,
        "wr": wi[:, A_COLS:rest_end].astype(BF16),
        "wml": wi[:, rest_end:].astype(BF16),
        "wqb": wqb.reshape(Q_LORA, N_HEADS * HEAD_QW).astype(BF16),
        "wkvb": w_kvb[l].astype(BF16),
        "q_norm_w": q_norm_w[l].reshape(1, Q_LORA), "kv_norm_w": kv_norm_w[l].reshape(1, KV_LORA),
        "conv3_w": conv3_w[l], "conv3_b": conv3_b[l].reshape(1, BRANCH_W),
        "pool_w": pool_w[l].astype(BF16), "pool_scale": pool_scale[l].reshape(1, BRANCH_W),
        "dw_w": dw_w[l], "dw_b": dw_b[l].reshape(1, BRANCH_W),
        "cln_w": cln_w[l].reshape(1, BRANCH_W), "cln_b": cln_b[l].reshape(1, BRANCH_W),
        "wb": w_bproj[l].astype(BF16), "wo": w_out[l].astype(BF16),
    }


def _rope_table(n_tokens):
    t = np.arange(n_tokens)
    half = QK_ROPE // 2
    inv = ROPE_BASE ** (-np.arange(0, half, 2, dtype=np.float32) / half)
    ang_r = (t // GRID_W).astype(np.float32)[:, None] * inv
    ang_c = (t % GRID_W).astype(np.float32)[:, None] * inv
    cos = np.concatenate([np.cos(ang_r), np.cos(ang_r), np.cos(ang_c), np.cos(ang_c)], axis=1)
    sin = np.concatenate([-np.sin(ang_r), np.sin(ang_r), -np.sin(ang_c), np.sin(ang_c)], axis=1)
    return jnp.asarray(np.concatenate([cos, sin], axis=1), dtype=F32)


def _stream(x, n_seq, seq, mods, layers, cond_map, final_w, chb, caches=None, rope_tab=None):
    tm_big, tm = 1024, 512
    mod3 = [mods[l].reshape(8, 1, 3 * D_MODEL) for l in range(DEPTH)]
    h = _hnorm(x, mod3[0], layers[0]["norm_w"], cond_map, tm)
    new_kvs = []
    for l in range(DEPTH):
        lw = layers[l]
        p = _inproj(h, lw["wr"], tm_big)
        if caches is None:
            y_a, new_kv = _attention(h, p, n_seq, seq, lw)
            new_kvs.append(new_kv)
        else:
            (y_a,) = _attention(h, p, n_seq, seq, lw, cache=caches[l], rope_tab=rope_tab)
        y_b = _short_conv(p, n_seq, seq, chb, lw)
        y_c = _pooling(p, n_seq, seq, lw)
        y_d = _conformer_conv(p, n_seq, seq, chb, lw)
        merged = _merge(h, (y_a, y_b, y_c, y_d), lw, tm)
        if l + 1 < DEPTH:
            x, h = _outproj(merged, lw["wo"], x, mod3[l], layers[l + 1]["norm_w"], cond_map, tm,
                            mod3_next=mod3[l + 1])
        else:
            x = _outproj(merged, lw["wo"], x, mod3[l], final_w, cond_map, tm)
    return x, new_kvs


def kernel(x_prompt, x_sample, cache_kv, c, c_ctx, w_ada, b_ada, norm_w, w_in, q_norm_w, w_qb, kv_norm_w,
           w_kvb, conv3_w, conv3_b, pool_w, pool_scale, dw_w, dw_b, cln_w, cln_b, w_bproj, w_out,
           final_norm_w):
    batch, seq, _ = x_prompt.shape
    dec_batch, dec_seq, _ = x_sample.shape
    assert 1 + dec_batch <= 8

    cond = jnp.concatenate([c_ctx[None, :], c, jnp.zeros((8 - 1 - dec_batch, D_MODEL), F32)], axis=0)
    mods = _modulation(cond, w_ada, b_ada)

    layers = []
    for l in range(DEPTH):
        lw = _pack_layer(l, w_in, q_norm_w, w_qb, kv_norm_w, w_kvb, conv3_w, conv3_b, pool_w, pool_scale,
                         dw_w, dw_b, cln_w, cln_b, w_bproj, w_out)
        lw["norm_w"] = norm_w[l].reshape(1, D_MODEL)
        layers.append(lw)
    final_w = final_norm_w.reshape(1, D_MODEL)

    xp = x_prompt.reshape(batch * seq, D_MODEL)
    y_p, new_kvs = _stream(xp, batch, seq, mods, layers, lambda i, tm: 0, final_w, BRANCH_W)

    caches = [jnp.concatenate([cache_kv[:, l], cache_kv[:, l, :, KV_LORA:]], axis=-1) for l in range(DEPTH)]
    xs = x_sample.reshape(dec_batch * dec_seq, D_MODEL)
    y_s, _ = _stream(xs, dec_batch, dec_seq, mods, layers, lambda i, tm: 1 + (i * tm) // dec_seq, final_w,
                     CH_TILE, caches=caches, rope_tab=_rope_table(dec_seq))

    return (y_p.reshape(batch, seq, D_MODEL), y_s.reshape(dec_batch, dec_seq, D_MODEL),
            jnp.stack(new_kvs, axis=1))
```

```python
import functools

import numpy as np
import jax
import jax.numpy as jnp
from jax import lax
from jax.experimental import pallas as pl
from jax.experimental.pallas import tpu as pltpu

F32 = jnp.float32
BF16 = jnp.bfloat16

D_MODEL = 2048
DEPTH = 2
GRID_W = 64
N_BRANCH = 4
BRANCH_W = D_MODEL // 2
QK_NOPE = 128
QK_ROPE = 64
V_HEAD = 128
N_HEADS = BRANCH_W // V_HEAD
Q_LORA = 512
KV_LORA = 256
ROPE_BASE = 10000.0
SHORT_K = 3
POOL_WINDOWS = (2, 4, 8, 16)
POOL_GW = BRANCH_W // len(POOL_WINDOWS)
CONF_K = 31
EPS = 1e-6

V7X_LANES = 128
V7X_SUBLANES = 8
V7X_VMEM_BYTES = 64 * 1024 * 1024
VMEM_LIMIT = V7X_VMEM_BYTES - 8 * 1024 * 1024

P_CHUNKS = 10
A_COLS = Q_LORA + KV_LORA + QK_ROPE
KV_W = 512
HEAD_QW = 256
ATT_QB = 256
CH_TILE = 256
ROW_CHUNK = 64
PAD3 = V7X_SUBLANES
PAD31 = 2 * V7X_SUBLANES
PADP = V7X_SUBLANES
INPROJ_TN = 2 * BRANCH_W
MERGE_TN = 512
MERGE_NT = D_MODEL // MERGE_TN


def _cparams(sem):
    return pltpu.CompilerParams(dimension_semantics=sem, vmem_limit_bytes=VMEM_LIMIT)


def _silu(x):
    return x * jax.nn.sigmoid(x)


def _dot(a, b):
    return jnp.dot(a, b, preferred_element_type=F32)


def _rms(x, w):
    return x * lax.rsqrt(jnp.mean(x * x, axis=-1, keepdims=True) + EPS) * w


def _modulate(x, nw, mod_row):
    shift = mod_row[:, 0:D_MODEL]
    scale = mod_row[:, D_MODEL:2 * D_MODEL]
    return _rms(x, nw) * (1.0 + scale) + shift


def _mod_kernel(c_ref, w_ref, b_ref, o_ref):
    s = _silu(c_ref[...]).astype(BF16)
    o_ref[0] = _dot(s, w_ref[0].astype(BF16)) + b_ref[0]


def _modulation(cond, w_ada, b_ada):
    tn = 768
    nt = (3 * D_MODEL) // tn
    return pl.pallas_call(
        _mod_kernel,
        grid=(DEPTH, nt),
        in_specs=[
            pl.BlockSpec((8, D_MODEL), lambda l, n: (0, 0)),
            pl.BlockSpec((1, D_MODEL, tn), lambda l, n: (l, 0, n)),
            pl.BlockSpec((1, 1, tn), lambda l, n: (l, 0, n)),
        ],
        out_specs=pl.BlockSpec((1, 8, tn), lambda l, n: (l, 0, n)),
        out_shape=jax.ShapeDtypeStruct((DEPTH, 8, 3 * D_MODEL), F32),
        compiler_params=_cparams(("arbitrary", "arbitrary")),
        name="adaln_mod",
    )(cond, w_ada, b_ada.reshape(DEPTH, 1, 3 * D_MODEL))


def _hnorm_kernel(x_ref, mod_ref, nw_ref, h_ref):
    h_ref[...] = _modulate(x_ref[...], nw_ref[...], mod_ref[0]).astype(BF16)


def _hnorm(x, mod3, norm_w, cond_map, tm):
    m = x.shape[0]
    return pl.pallas_call(
        _hnorm_kernel,
        grid=(m // tm,),
        in_specs=[pl.BlockSpec((tm, D_MODEL), lambda i: (i, 0)),
                  pl.BlockSpec((1, 1, 3 * D_MODEL), lambda i: (cond_map(i, tm), 0, 0)),
                  pl.BlockSpec((1, D_MODEL), lambda i: (0, 0))],
        out_specs=pl.BlockSpec((tm, D_MODEL), lambda i: (i, 0)),
        out_shape=jax.ShapeDtypeStruct((m, D_MODEL), BF16),
        compiler_params=_cparams(("arbitrary",)),
        name="mod_norm",
    )(x, mod3, norm_w)


def _inproj_kernel(h_ref, w_ref, p_ref):
    p_ref[...] = _dot(h_ref[...], w_ref[...])


def _inproj(h, wr, tm):
    m = h.shape[0]
    return pl.pallas_call(
        _inproj_kernel,
        grid=(P_CHUNKS * BRANCH_W // INPROJ_TN, m // tm),
        in_specs=[pl.BlockSpec((tm, D_MODEL), lambda n, i: (i, 0)),
                  pl.BlockSpec((D_MODEL, INPROJ_TN), lambda n, i: (0, n))],
        out_specs=pl.BlockSpec((tm, INPROJ_TN), lambda n, i: (i, n)),
        out_shape=jax.ShapeDtypeStruct((m, P_CHUNKS * BRANCH_W), F32),
        compiler_params=_cparams(("arbitrary", "arbitrary")),
        name="inproj",
    )(h, wr)


def _attn_kernel(*refs, seq, n_cache, rope):
    if rope:
        (hq_ref, hkv_ref, ga_ref, cache_ref, tq_ref, tk_ref, waq_ref, wakv_ref, qnw_ref, wqb_ref, kvnw_ref,
         wkvb_ref, y_ref, kvup_scr, kpe_scr) = refs
    else:
        (hq_ref, hkv_ref, ga_ref, waq_ref, wakv_ref, qnw_ref, wqb_ref, kvnw_ref, wkvb_ref,
         y_ref, newkv_ref, kvup_scr, kpe_scr) = refs

    @pl.when(pl.program_id(1) == 0)
    def _():
        pkv = _dot(hkv_ref[...], wakv_ref[...])
        c_kv = _rms(pkv[:, 0:KV_LORA], kvnw_ref[...])
        kpe2 = pkv[:, KV_LORA:KV_LORA + V7X_LANES]
        if rope:
            z = kpe2 * tk_ref[...]
            kpe_own = z + pltpu.roll(z, QK_ROPE, axis=1)
            kvup_scr[0:n_cache, :] = _dot(cache_ref[0, :, 0:KV_LORA].astype(BF16), wkvb_ref[...]).astype(BF16)
            kpe_scr[0:n_cache, :] = cache_ref[0, :, KV_LORA:KV_LORA + V7X_LANES].astype(BF16)
        else:
            lane = lax.broadcasted_iota(jnp.int32, kpe2.shape, 1)
            kpe_own = jnp.where(lane < QK_ROPE, kpe2, 0.0)
            newkv_ref[0, :, 0:KV_LORA] = c_kv
            newkv_ref[0, :, KV_LORA:KV_LORA + QK_ROPE] = kpe2[:, 0:QK_ROPE]
        kvup_scr[n_cache:n_cache + seq, :] = _dot(c_kv.astype(BF16), wkvb_ref[...]).astype(BF16)
        kpe_scr[n_cache:n_cache + seq, :] = kpe_own.astype(BF16)

    qa = _dot(hq_ref[...], waq_ref[...])
    qn = _rms(qa, qnw_ref[...]).astype(BF16)
    q = _dot(qn, wqb_ref[...])
    sm_scale = float(QK_NOPE + QK_ROPE) ** -0.5
    kpe = kpe_scr[...]
    for h in range(N_HEADS):
        c0 = h * HEAD_QW
        q_nope = q[:, c0:c0 + QK_NOPE]
        q_pe = q[:, c0 + QK_NOPE:c0 + HEAD_QW]
        if rope:
            q_pe = q_pe * tq_ref[...]
        qh = jnp.concatenate([q_nope, q_pe], axis=1).astype(BF16)
        kh = jnp.concatenate([kvup_scr[:, c0:c0 + QK_NOPE], kpe], axis=1)
        s = lax.dot_general(qh, kh, (((1,), (1,)), ((), ())), preferred_element_type=F32) * sm_scale
        s = s - jnp.max(s, axis=-1, keepdims=True)
        e = jnp.exp(s)
        p = (e * (1.0 / jnp.sum(e, axis=-1, keepdims=True))).astype(BF16)
        o = _dot(p, kvup_scr[:, c0 + QK_NOPE:c0 + HEAD_QW])
        g = ga_ref[:, h * V_HEAD:(h + 1) * V_HEAD]
        y_ref[:, h * V_HEAD:(h + 1) * V_HEAD] = (o * _silu(g)).astype(BF16)


def _attention(h, p, n_seq, seq, lw, cache=None, rope_tab=None):
    m = n_seq * seq
    nqb = seq // ATT_QB
    rope = cache is not None
    n_cache = cache.shape[1] if rope else 0
    in_specs = [
        pl.BlockSpec((ATT_QB, D_MODEL), lambda b, q: (b * nqb + q, 0)),
        pl.BlockSpec((seq, D_MODEL), lambda b, q: (b, 0)),
        pl.BlockSpec((ATT_QB, BRANCH_W), lambda b, q: (b * nqb + q, 0)),
    ]
    args = [h, h, p]
    if rope:
        in_specs += [
            pl.BlockSpec((1, n_cache, KV_LORA + V7X_LANES), lambda b, q: (b, 0, 0)),
            pl.BlockSpec((ATT_QB, V7X_LANES), lambda b, q: (q, 0)),
            pl.BlockSpec((seq, V7X_LANES), lambda b, q: (0, 0)),
        ]
        args += [cache, rope_tab, rope_tab]
    in_specs += [
        pl.BlockSpec((D_MODEL, Q_LORA), lambda b, q: (0, 0)),
        pl.BlockSpec((D_MODEL, KV_W), lambda b, q: (0, 0)),
        pl.BlockSpec((1, Q_LORA), lambda b, q: (0, 0)),
        pl.BlockSpec((Q_LORA, N_HEADS * HEAD_QW), lambda b, q: (0, 0)),
        pl.BlockSpec((1, KV_LORA), lambda b, q: (0, 0)),
        pl.BlockSpec((KV_LORA, N_HEADS * HEAD_QW), lambda b, q: (0, 0)),
    ]
    args += [lw["wa_q"], lw["wa_kv"], lw["q_norm_w"], lw["wqb"], lw["kv_norm_w"], lw["wkvb"]]
    out_specs = [pl.BlockSpec((ATT_QB, BRANCH_W), lambda b, q: (b * nqb + q, 0))]
    out_shape = [jax.ShapeDtypeStruct((m, BRANCH_W), BF16)]
    if not rope:
        out_specs.append(pl.BlockSpec((1, seq, KV_LORA + QK_ROPE), lambda b, q: (b, 0, 0)))
        out_shape.append(jax.ShapeDtypeStruct((n_seq, seq, KV_LORA + QK_ROPE), F32))
    lk = n_cache + seq
    return pl.pallas_call(
        functools.partial(_attn_kernel, seq=seq, n_cache=n_cache, rope=rope),
        grid=(n_seq, nqb),
        in_specs=in_specs,
        out_specs=out_specs,
        out_shape=out_shape,
        scratch_shapes=[pltpu.VMEM((lk, N_HEADS * HEAD_QW), BF16), pltpu.VMEM((lk, V7X_LANES), BF16)],
        compiler_params=_cparams(("arbitrary", "arbitrary")),
        name="mla_attention",
    )(*args)


def _dwconv_rows(pad_ref, w_ref, b_ref, cols, seq, taps, first_row, emit):
    groups = {}
    for k in range(taps):
        a, r = divmod(first_row + k, V7X_SUBLANES)
        groups.setdefault(r, []).append((k, a))
    for r0 in range(0, seq, ROW_CHUNK):
        acc = None
        for r, members in sorted(groups.items()):
            rows = ROW_CHUNK + (V7X_SUBLANES if r else 0)
            z = None
            for k, a in members:
                base = r0 + a * V7X_SUBLANES
                term = pad_ref[base:base + rows, :] * w_ref[k:k + 1, cols]
                z = term if z is None else z + term
            z = z[r:r + ROW_CHUNK, :]
            acc = z if acc is None else acc + z
        emit(r0, acc + b_ref[:, cols])


def _sconv_kernel(bg_ref, cg_ref, xc_ref, gb_ref, w_ref, b_ref, y_ref, pad_scr, *, seq, chb):
    zeros = jnp.zeros((PAD3, CH_TILE), F32)
    pad_scr[0:PAD3, :] = zeros
    pad_scr[PAD3 + seq:PAD3 + seq + PAD3, :] = zeros
    for c0 in range(0, chb, CH_TILE):
        cols = slice(c0, c0 + CH_TILE)
        pad_scr[PAD3:PAD3 + seq, :] = cg_ref[:, cols] * xc_ref[:, cols]

        def emit(r0, conv, cols=cols):
            rows = slice(r0, r0 + ROW_CHUNK)
            y_ref[rows, cols] = (bg_ref[rows, cols] * conv * _silu(gb_ref[rows, cols])).astype(BF16)

        _dwconv_rows(pad_scr, w_ref, b_ref, cols, seq, SHORT_K, PAD3 - (SHORT_K - 1) // 2, emit)


def _short_conv(p, n_seq, seq, chb, lw):
    m = n_seq * seq
    ct = BRANCH_W // chb

    def chunk(c):
        return pl.BlockSpec((seq, chb), lambda b, j: (b, c * ct + j))

    return pl.pallas_call(
        functools.partial(_sconv_kernel, seq=seq, chb=chb),
        grid=(n_seq, ct),
        in_specs=[chunk(1), chunk(2), chunk(3), chunk(4),
                  pl.BlockSpec((SHORT_K, chb), lambda b, j: (0, j)),
                  pl.BlockSpec((1, chb), lambda b, j: (0, j))],
        out_specs=pl.BlockSpec((seq, chb), lambda b, j: (b, j)),
        out_shape=jax.ShapeDtypeStruct((m, BRANCH_W), BF16),
        scratch_shapes=[pltpu.VMEM((seq + 2 * PAD3, CH_TILE), F32)],
        compiler_params=_cparams(("arbitrary", "arbitrary")),
        name="short_conv",
    )(p, p, p, p, lw["conv3_w"], lw["conv3_b"])


def _pool_kernel(xp_ref, gc_ref, w_ref, sc_ref, y_ref, pad_scr, pooled_scr, *, seq):
    zeros = jnp.zeros((PADP, BRANCH_W), F32)
    pad_scr[0:PADP, :] = zeros
    pad_scr[PADP + seq:PADP + seq + PADP, :] = zeros
    pad_scr[PADP:PADP + seq, :] = xp_ref[...]
    for g, win in enumerate(POOL_WINDOWS):
        cols = slice(g * POOL_GW, (g + 1) * POOL_GW)
        hw = win // 2
        for r0 in range(0, seq, ROW_CHUNK):
            base = PADP - hw + r0
            tot = pad_scr[base:base + ROW_CHUNK, cols]
            for d in range(1, win):
                tot = tot + pad_scr[base + d:base + d + ROW_CHUNK, cols]
            t = r0 + lax.broadcasted_iota(jnp.int32, (ROW_CHUNK, POOL_GW), 0)
            cnt = (jnp.minimum(t + hw, seq) - jnp.maximum(t - hw, 0)).astype(F32)
            pooled = tot / cnt - xp_ref[r0:r0 + ROW_CHUNK, cols]
            pooled_scr[r0:r0 + ROW_CHUNK, cols] = pooled.astype(BF16)
        y = _dot(pooled_scr[:, cols], w_ref[g]) * sc_ref[:, cols] * _silu(gc_ref[:, cols])
        y_ref[:, cols] = y.astype(BF16)


def _pooling(p, n_seq, seq, lw):
    m = n_seq * seq
    return pl.pallas_call(
        functools.partial(_pool_kernel, seq=seq),
        grid=(n_seq,),
        in_specs=[pl.BlockSpec((seq, BRANCH_W), lambda b: (b, 5)),
                  pl.BlockSpec((seq, BRANCH_W), lambda b: (b, 6)),
                  pl.BlockSpec((len(POOL_WINDOWS), POOL_GW, POOL_GW), lambda b: (0, 0, 0)),
                  pl.BlockSpec((1, BRANCH_W), lambda b: (0, 0))],
        out_specs=pl.BlockSpec((seq, BRANCH_W), lambda b: (b, 0)),
        out_shape=jax.ShapeDtypeStruct((m, BRANCH_W), BF16),
        scratch_shapes=[pltpu.VMEM((seq + 2 * PADP, BRANCH_W), F32),
                        pltpu.VMEM((seq, BRANCH_W), BF16)],
        compiler_params=_cparams(("arbitrary",)),
        name="ms_pool",
    )(p, p, lw["pool_w"], lw["pool_scale"])


def _cconv_kernel(ga_ref, gb_ref, gd_ref, w_ref, b_ref, lnw_ref, lnb_ref, y_ref, pad_scr, conv_scr, *, seq, chb):
    j = pl.program_id(1)
    sub = chb // CH_TILE
    zeros = jnp.zeros((PAD31, CH_TILE), F32)
    pad_scr[0:PAD31, :] = zeros
    pad_scr[PAD31 + seq:PAD31 + seq + PAD31, :] = zeros
    for s in range(sub):
        cols = slice(s * CH_TILE, (s + 1) * CH_TILE)
        pad_scr[PAD31:PAD31 + seq, :] = ga_ref[:, cols] * jax.nn.sigmoid(gb_ref[:, cols])

        def emit(r0, conv, s=s):
            conv_scr[j * sub + s, r0:r0 + ROW_CHUNK, :] = conv

        _dwconv_rows(pad_scr, w_ref, b_ref, cols, seq, CONF_K, PAD31 - (CONF_K - 1) // 2, emit)

    n_tiles = BRANCH_W // CH_TILE

    @pl.when(j == BRANCH_W // chb - 1)
    def _():
        for r0 in range(0, seq, ROW_CHUNK):
            rows = slice(r0, r0 + ROW_CHUNK)
            u = jnp.concatenate([conv_scr[c, rows, :] for c in range(n_tiles)], axis=1)
            mu = jnp.mean(u, axis=-1, keepdims=True)
            var = jnp.mean(jnp.square(u - mu), axis=-1, keepdims=True)
            v = (u - mu) * lax.rsqrt(var + EPS) * lnw_ref[...] + lnb_ref[...]
            y_ref[rows, :] = (_silu(v) * _silu(gd_ref[rows, :])).astype(BF16)


def _conformer_conv(p, n_seq, seq, chb, lw):
    m = n_seq * seq
    ct = BRANCH_W // chb
    return pl.pallas_call(
        functools.partial(_cconv_kernel, seq=seq, chb=chb),
        grid=(n_seq, ct),
        in_specs=[pl.BlockSpec((seq, chb), lambda b, j: (b, 7 * ct + j)),
                  pl.BlockSpec((seq, chb), lambda b, j: (b, 8 * ct + j)),
                  pl.BlockSpec((seq, BRANCH_W), lambda b, j: (b, 9)),
                  pl.BlockSpec((CONF_K, chb), lambda b, j: (0, j)),
                  pl.BlockSpec((1, chb), lambda b, j: (0, j)),
                  pl.BlockSpec((1, BRANCH_W), lambda b, j: (0, 0)),
                  pl.BlockSpec((1, BRANCH_W), lambda b, j: (0, 0))],
        out_specs=pl.BlockSpec((seq, BRANCH_W), lambda b, j: (b, 0)),
        out_shape=jax.ShapeDtypeStruct((m, BRANCH_W), BF16),
        scratch_shapes=[pltpu.VMEM((seq + 2 * PAD31, CH_TILE), F32),
                        pltpu.VMEM((BRANCH_W // CH_TILE, seq, CH_TILE), F32)],
        compiler_params=_cparams(("arbitrary", "arbitrary")),
        name="conformer_conv",
    )(p, p, p, lw["dw_w"], lw["dw_b"], lw["cln_w"], lw["cln_b"])


def _merge_kernel(h_ref, ya_ref, yb_ref, yc_ref, yd_ref, wm0_ref, wm1_ref, wm2_ref, wm3_ref, wb_ref, o_ref):
    h = h_ref[...]
    acc = None
    for i, (y_ref, wm_ref) in enumerate(((ya_ref, wm0_ref), (yb_ref, wm1_ref), (yc_ref, wm2_ref),
                                         (yd_ref, wm3_ref))):
        gate = jax.nn.sigmoid(_dot(h, wm_ref[...]))
        term = gate * _dot(y_ref[...], wb_ref[i])
        acc = term if acc is None else acc + term
    o_ref[...] = acc.astype(BF16)


def _merge(h, ys, lw, tm):
    m = h.shape[0]
    y_spec = pl.BlockSpec((tm, BRANCH_W), lambda n, i: (i, 0))

    def wm_spec(branch):
        return pl.BlockSpec((D_MODEL, MERGE_TN), lambda n, i: (0, branch * MERGE_NT + n))

    return pl.pallas_call(
        _merge_kernel,
        grid=(MERGE_NT, m // tm),
        in_specs=[pl.BlockSpec((tm, D_MODEL), lambda n, i: (i, 0)),
                  y_spec, y_spec, y_spec, y_spec,
                  wm_spec(0), wm_spec(1), wm_spec(2), wm_spec(3),
                  pl.BlockSpec((N_BRANCH, BRANCH_W, MERGE_TN), lambda n, i: (0, 0, n))],
        out_specs=pl.BlockSpec((tm, MERGE_TN), lambda n, i: (i, n)),
        out_shape=jax.ShapeDtypeStruct((m, D_MODEL), BF16),
        compiler_params=_cparams(("arbitrary", "arbitrary")),
        name="merge",
    )(h, *ys, lw["wml"], lw["wml"], lw["wml"], lw["wml"], lw["wb"])


def _outproj_kernel(mg_ref, wo_ref, x_ref, mod_ref, nw_ref, *rest, last):
    gate = mod_ref[0, :, 2 * D_MODEL:3 * D_MODEL]
    xn = x_ref[...] + gate * _dot(mg_ref[...], wo_ref[...])
    if last:
        (y_ref,) = rest
        y_ref[...] = _rms(xn, nw_ref[...])
    else:
        modn_ref, xo_ref, h_ref = rest
        xo_ref[...] = xn
        h_ref[...] = _modulate(xn, nw_ref[...], modn_ref[0]).astype(BF16)


def _outproj(merged, wo, x, mod3, nw, cond_map, tm, mod3_next=None):
    m = x.shape[0]
    last = mod3_next is None
    row = pl.BlockSpec((tm, D_MODEL), lambda i: (i, 0))
    mod_spec = pl.BlockSpec((1, 1, 3 * D_MODEL), lambda i: (cond_map(i, tm), 0, 0))
    in_specs = [row, pl.BlockSpec((D_MODEL, D_MODEL), lambda i: (0, 0)), row, mod_spec,
                pl.BlockSpec((1, D_MODEL), lambda i: (0, 0))]
    args = [merged, wo, x, mod3, nw]
    if last:
        out_specs = row
        out_shape = jax.ShapeDtypeStruct((m, D_MODEL), F32)
    else:
        in_specs.append(mod_spec)
        args.append(mod3_next)
        out_specs = [row, row]
        out_shape = [jax.ShapeDtypeStruct((m, D_MODEL), F32), jax.ShapeDtypeStruct((m, D_MODEL), BF16)]
    return pl.pallas_call(
        functools.partial(_outproj_kernel, last=last),
        grid=(m // tm,),
        in_specs=in_specs,
        out_specs=out_specs,
        out_shape=out_shape,
        compiler_params=_cparams(("arbitrary",)),
        name="outproj",
    )(*args)


def _rope_swap_cols(w):
    q = QK_ROPE // 4
    return jnp.concatenate([w[..., q:2 * q], w[..., 0:q], w[..., 3 * q:4 * q], w[..., 2 * q:3 * q]], axis=-1)


def _pack_layer(l, w_in, q_norm_w, w_qb, kv_norm_w, w_kvb, conv3_w, conv3_b, pool_w, pool_scale,
                dw_w, dw_b, cln_w, cln_b, w_bproj, w_out):
    wi = w_in[l]
    rest_end = A_COLS + P_CHUNKS * BRANCH_W
    k_pe_w = wi[:, Q_LORA + KV_LORA:A_COLS]
    wa_kv = jnp.concatenate([wi[:, Q_LORA:A_COLS], _rope_swap_cols(k_pe_w),
                             jnp.zeros((D_MODEL, KV_W - KV_LORA - 2 * QK_ROPE), F32)], axis=1).astype(BF16)
    wq = w_qb[l].reshape(Q_LORA, N_HEADS, QK_NOPE + QK_ROPE)
    wqb = jnp.concatenate([wq, _rope_swap_cols(wq[..., QK_NOPE:])], axis=-1)
    return {
        "wa_q": wi[:, :Q_LORA].astype(BF16), "wa_kv": wa_kv,
        "wr": wi[:, A_COLS:rest_end].astype(BF16),
        "wml": wi[:, rest_end:].astype(BF16),
        "wqb": wqb.reshape(Q_LORA, N_HEADS * HEAD_QW).astype(BF16),
        "wkvb": w_kvb[l].astype(BF16),
        "q_norm_w": q_norm_w[l].reshape(1, Q_LORA), "kv_norm_w": kv_norm_w[l].reshape(1, KV_LORA),
        "conv3_w": conv3_w[l], "conv3_b": conv3_b[l].reshape(1, BRANCH_W),
        "pool_w": pool_w[l].astype(BF16), "pool_scale": pool_scale[l].reshape(1, BRANCH_W),
        "dw_w": dw_w[l], "dw_b": dw_b[l].reshape(1, BRANCH_W),
        "cln_w": cln_w[l].reshape(1, BRANCH_W), "cln_b": cln_b[l].reshape(1, BRANCH_W),
        "wb": w_bproj[l].astype(BF16), "wo": w_out[l].astype(BF16),
    }


def _rope_table(n_tokens):
    t = np.arange(n_tokens)
    half = QK_ROPE // 2
    inv = ROPE_BASE ** (-np.arange(0, half, 2, dtype=np.float32) / half)
    ang_r = (t // GRID_W).astype(np.float32)[:, None] * inv
    ang_c = (t % GRID_W).astype(np.float32)[:, None] * inv
    cos = np.concatenate([np.cos(ang_r), np.cos(ang_r), np.cos(ang_c), np.cos(ang_c)], axis=1)
    sin = np.concatenate([-np.sin(ang_r), np.sin(ang_r), -np.sin(ang_c), np.sin(ang_c)], axis=1)
    return jnp.asarray(np.concatenate([cos, sin], axis=1), dtype=F32)


def _stream(x, n_seq, seq, mods, layers, cond_map, final_w, chb, caches=None, rope_tab=None):
    tm_big, tm = 1024, 512
    mod3 = [mods[l].reshape(8, 1, 3 * D_MODEL) for l in range(DEPTH)]
    h = _hnorm(x, mod3[0], layers[0]["norm_w"], cond_map, tm)
    new_kvs = []
    for l in range(DEPTH):
        lw = layers[l]
        p = _inproj(h, lw["wr"], tm_big)
        if caches is None:
            y_a, new_kv = _attention(h, p, n_seq, seq, lw)
            new_kvs.append(new_kv)
        else:
            (y_a,) = _attention(h, p, n_seq, seq, lw, cache=caches[l], rope_tab=rope_tab)
        y_b = _short_conv(p, n_seq, seq, chb, lw)
        y_c = _pooling(p, n_seq, seq, lw)
        y_d = _conformer_conv(p, n_seq, seq, chb, lw)
        merged = _merge(h, (y_a, y_b, y_c, y_d), lw, tm)
        if l + 1 < DEPTH:
            x, h = _outproj(merged, lw["wo"], x, mod3[l], layers[l + 1]["norm_w"], cond_map, tm,
                            mod3_next=mod3[l + 1])
        else:
            x = _outproj(merged, lw["wo"], x, mod3[l], final_w, cond_map, tm)
    return x, new_kvs


def kernel(x_prompt, x_sample, cache_kv, c, c_ctx, w_ada, b_ada, norm_w, w_in, q_norm_w, w_qb, kv_norm_w,
           w_kvb, conv3_w, conv3_b, pool_w, pool_scale, dw_w, dw_b, cln_w, cln_b, w_bproj, w_out,
           final_norm_w):
    batch, seq, _ = x_prompt.shape
    dec_batch, dec_seq, _ = x_sample.shape
    assert 1 + dec_batch <= 8

    cond = jnp.concatenate([c_ctx[None, :], c, jnp.zeros((8 - 1 - dec_batch, D_MODEL), F32)], axis=0)
    mods = _modulation(cond, w_ada, b_ada)

    layers = []
    for l in range(DEPTH):
        lw = _pack_layer(l, w_in, q_norm_w, w_qb, kv_norm_w, w_kvb, conv3_w, conv3_b, pool_w, pool_scale,
                         dw_w, dw_b, cln_w, cln_b, w_bproj, w_out)
        lw["norm_w"] = norm_w[l].reshape(1, D_MODEL)
        layers.append(lw)
    final_w = final_norm_w.reshape(1, D_MODEL)

    xp = x_prompt.reshape(batch * seq, D_MODEL)
    y_p, new_kvs = _stream(xp, batch, seq, mods, layers, lambda i, tm: 0, final_w, BRANCH_W)

    caches = [jnp.concatenate([cache_kv[:, l], cache_kv[:, l, :, KV_LORA:]], axis=-1) for l in range(DEPTH)]
    xs = x_sample.reshape(dec_batch * dec_seq, D_MODEL)
    y_s, _ = _stream(xs, dec_batch, dec_seq, mods, layers, lambda i, tm: 1 + (i * tm) // dec_seq, final_w,
                     CH_TILE, caches=caches, rope_tab=_rope_table(dec_seq))

    return (y_p.reshape(batch, seq, D_MODEL), y_s.reshape(dec_batch, dec_seq, D_MODEL),
            jnp.stack(new_kvs, axis=1))
```
